```python
import jax, jax.numpy as jnp
from jax import lax
import numpy as np

D_MODEL = 2048
BATCH = 4
SEQ = 4096
DEPTH = 1

RW_HEADS = 16
RW_HEAD_DIM = 64
RW_WIDTH = RW_HEADS * RW_HEAD_DIM
DECAY_LORA = 64
ICLR_LORA = 64
LNX_EPS = 64e-5
AT_HEADS = 16
AT_KV_HEADS = 4
AT_HEAD_DIM = 64
AT_GROUP = AT_HEADS // AT_KV_HEADS
AT_WIDTH = AT_HEADS * AT_HEAD_DIM
AT_KV_WIDTH = AT_KV_HEADS * AT_HEAD_DIM
WINDOW = 128
BLOCK = 128
NEG_INF = -1e30
N_BRANCH = 2
NORM_EPS = 1e-6
RW_COLS = 4 * RW_WIDTH + DECAY_LORA + ICLR_LORA
AT_COLS = 2 * AT_WIDTH + 2 * AT_KV_WIDTH
GATE_COLS = N_BRANCH * D_MODEL
IN_COLS = RW_COLS + AT_COLS + GATE_COLS

kernel_name = "hybrid_rwkv7_swa_sink_gated_merge"


def rms_norm(x, g, eps=NORM_EPS):
    xf = x.astype(jnp.float32)
    return xf * lax.rsqrt(jnp.mean(xf * xf, axis=-1, keepdims=True) + eps) * g.astype(jnp.float32)


def rwkv7_scan(r, w, k, v, a, b):
    Bsz, T, H, N = r.shape

    def step(S, inp):
        r_t, w_t, k_t, v_t, a_t, b_t = inp
        sa = jnp.einsum('bhij,bhj->bhi', S, a_t)
        S = S * w_t[:, :, None, :] + sa[..., None] * b_t[:, :, None, :] + v_t[..., None] * k_t[:, :, None, :]
        return S, jnp.einsum('bhij,bhj->bhi', S, r_t)

    xs = tuple(jnp.moveaxis(t.astype(jnp.float32), 1, 0) for t in (r, w, k, v, a, b))
    S0 = jnp.zeros((Bsz, H, N, N), jnp.float32)
    _, y = lax.scan(step, S0, xs)
    return jnp.moveaxis(y, 0, 1)


def rwkv7_branch(p, mu, w0, w2, a0, a2, k_k, k_a, r_k, lnx_w, lnx_b):
    Bsz, T, _ = p.shape
    H, N = RW_HEADS, RW_HEAD_DIM
    p = p.astype(jnp.float32)
    p_prev = jnp.pad(p, ((0, 0), (1, 0), (0, 0)))[:, :-1]
    p = p + (p_prev - p) * mu
    r, k, v, gate, wd, ad = jnp.split(
        p, [RW_WIDTH, 2 * RW_WIDTH, 3 * RW_WIDTH, 4 * RW_WIDTH, 4 * RW_WIDTH + DECAY_LORA], axis=-1)
    logw = -jax.nn.softplus(-(w0 + jnp.tanh(wd) @ w2)) - 0.5
    decay = jnp.exp(-jnp.exp(logw))
    a = jax.nn.sigmoid(a0 + ad @ a2)
    kk = (k * k_k).reshape(Bsz, T, H, N)
    kk = kk / jnp.maximum(jnp.sqrt(jnp.sum(kk * kk, axis=-1, keepdims=True)), 1e-12)
    k = k * (1.0 + (a - 1.0) * k_a)
    hs = lambda t: t.reshape(Bsz, T, H, N)
    r_h, k_h, v_h, a_h = hs(r), hs(k), hs(v), hs(a)
    y = rwkv7_scan(r_h, hs(decay), k_h, v_h, -kk, kk * a_h)
    mean = jnp.mean(y, axis=-1, keepdims=True)
    var = jnp.mean(jnp.square(y - mean), axis=-1, keepdims=True)
    y = ((y - mean) * lax.rsqrt(var + LNX_EPS)).reshape(Bsz, T, RW_WIDTH) * lnx_w + lnx_b
    bonus = jnp.sum(r_h * k_h * r_k, axis=-1, keepdims=True) * v_h
    y = y + bonus.reshape(Bsz, T, RW_WIDTH)
    return y * jax.nn.silu(gate)


def swa_sink_branch(p, q_norm, k_norm, sinks):
    Bsz, T, _ = p.shape
    q, gate, k, v = jnp.split(p, [AT_WIDTH, 2 * AT_WIDTH, 2 * AT_WIDTH + AT_KV_WIDTH], axis=-1)
    q = rms_norm(q.reshape(Bsz, T, AT_HEADS, AT_HEAD_DIM), q_norm)
    k = rms_norm(k.reshape(Bsz, T, AT_KV_HEADS, AT_HEAD_DIM), k_norm)
    v = v.reshape(Bsz, T, AT_KV_HEADS, AT_HEAD_DIM).astype(jnp.float32)
    nb = T // BLOCK
    qb = q.reshape(Bsz, nb, BLOCK, AT_KV_HEADS, AT_GROUP, AT_HEAD_DIM)

    def band(t):
        tb = t.reshape(Bsz, nb, BLOCK, AT_KV_HEADS, AT_HEAD_DIM)
        prev = jnp.pad(tb, ((0, 0), (1, 0), (0, 0), (0, 0), (0, 0)))[:, :-1]
        return jnp.concatenate([prev, tb], axis=2)

    kb, vb = band(k), band(v)
    s = jnp.einsum('bnqkgd,bnskd->bnkgqs', qb, kb) * (AT_HEAD_DIM ** -0.5)
    qi = jnp.arange(BLOCK)[:, None]
    si = jnp.arange(2 * BLOCK)[None, :]
    blk = jnp.arange(nb)[:, None, None]
    diff = qi + BLOCK - si
    allowed = (diff >= 0) & (diff < WINDOW) & (blk * BLOCK + si - BLOCK >= 0)
    s = jnp.where(allowed[None, :, None, None, :, :], s, NEG_INF)
    sink = jnp.broadcast_to(
        sinks.astype(jnp.float32).reshape(AT_KV_HEADS, AT_GROUP)[None, None, :, :, None, None],
        s.shape[:-1] + (1,))
    prob = jax.nn.softmax(jnp.concatenate([s, sink], axis=-1), axis=-1)[..., :-1]
    o = jnp.einsum('bnkgqs,bnskd->bnqkgd', prob, vb).reshape(Bsz, T, AT_WIDTH)
    return o * jax.nn.silu(gate.astype(jnp.float32))


def setup_inputs(seed: int = 0) -> dict:
    key = jax.random.key(seed)
    ks = jax.random.split(key, 24)
    L, D = DEPTH, D_MODEL
    nrm = lambda k, shape, s: jax.random.normal(k, shape, jnp.float32) * s
    return {
        "x": nrm(ks[0], (BATCH, SEQ, D), 1.0),
        "c": nrm(ks[1], (BATCH, D), 1.0),
        "ada_w": nrm(ks[2], (L, D, 3 * D), 0.5 * D ** -0.5),
        "ada_b": nrm(ks[3], (L, 3 * D), 0.01),
        "norm_g": 1.0 + nrm(ks[4], (L, D), 0.1),
        "w_in": nrm(ks[5], (L, D, IN_COLS), D ** -0.5),
        "mu_shift": jax.random.uniform(ks[6], (L, RW_COLS), jnp.float32),
        "w0": nrm(ks[7], (L, RW_WIDTH), 0.5),
        "w2": nrm(ks[8], (L, DECAY_LORA, RW_WIDTH), 0.5 * DECAY_LORA ** -0.5),
        "a0": nrm(ks[9], (L, RW_WIDTH), 0.5),
        "a2": nrm(ks[10], (L, ICLR_LORA, RW_WIDTH), 0.5 * ICLR_LORA ** -0.5),
        "k_k": 0.85 + nrm(ks[11], (L, RW_WIDTH), 0.05),
        "k_a": 1.0 + nrm(ks[12], (L, RW_WIDTH), 0.05),
        "r_k": nrm(ks[13], (L, RW_HEADS, RW_HEAD_DIM), 0.1),
        "lnx_w": 1.0 + nrm(ks[14], (L, RW_WIDTH), 0.1),
        "lnx_b": nrm(ks[15], (L, RW_WIDTH), 0.01),
        "q_norm": 1.0 + nrm(ks[16], (L, AT_HEAD_DIM), 0.1),
        "k_norm": 1.0 + nrm(ks[17], (L, AT_HEAD_DIM), 0.1),
        "sinks": nrm(ks[18], (L, AT_HEADS), 0.5),
        "w_up_r": nrm(ks[19], (L, RW_WIDTH, D), RW_WIDTH ** -0.5),
        "w_up_a": nrm(ks[20], (L, AT_WIDTH, D), AT_WIDTH ** -0.5),
        "w_o": nrm(ks[21], (L, D, D), D ** -0.5),
    }


def reference(x, c, ada_w, ada_b, norm_g, w_in, mu_shift, w0, w2, a0, a2, k_k, k_a, r_k,
              lnx_w, lnx_b, q_norm, k_norm, sinks, w_up_r, w_up_a, w_o):
    for l in range(DEPTH):
        mod = c @ ada_w[l] + ada_b[l]
        shift, scale, gate = jnp.split(mod, 3, axis=-1)
        h = rms_norm(x, norm_g[l]) * (1.0 + scale[:, None, :]) + shift[:, None, :]
        h = h.astype(x.dtype)
        p = h @ w_in[l]
        p_r, p_a, p_g = jnp.split(p, [RW_COLS, RW_COLS + AT_COLS], axis=-1)
        y_r = rwkv7_branch(p_r, mu_shift[l], w0[l], w2[l], a0[l], a2[l], k_k[l], k_a[l], r_k[l],
                           lnx_w[l], lnx_b[l]).astype(x.dtype)
        y_a = swa_sink_branch(p_a, q_norm[l], k_norm[l], sinks[l]).astype(x.dtype)
        g_r, g_a = jnp.split(jax.nn.sigmoid(p_g), N_BRANCH, axis=-1)
        m = g_r * (y_r @ w_up_r[l]) + g_a * (y_a @ w_up_a[l])
        x = x + gate[:, None, :] * (m @ w_o[l])
    return x
```

```python
import functools
import math

import jax
import jax.numpy as jnp
from jax import lax
from jax.experimental import pallas as pl
from jax.experimental.pallas import tpu as pltpu

F32 = jnp.float32
BF16 = jnp.bfloat16

RW_HEADS = 16
RW_HEAD_DIM = 64
RW_WIDTH = RW_HEADS * RW_HEAD_DIM
DECAY_LORA = 64
ICLR_LORA = 64
LORA_COLS = DECAY_LORA + ICLR_LORA
LNX_EPS = 64e-5
AT_HEADS = 16
AT_KV_HEADS = 4
AT_HEAD_DIM = 64
AT_GROUP = AT_HEADS // AT_KV_HEADS
AT_WIDTH = AT_HEADS * AT_HEAD_DIM
AT_KV_WIDTH = AT_KV_HEADS * AT_HEAD_DIM
WINDOW = 128
BLOCK = 128
NEG_INF = -1e30
NORM_EPS = 1e-6

CHUNK = 64
SEG_LANES = 256
VMEM_LIMIT = 48 * 1024 * 1024


def _cparams(sem):
    return pltpu.CompilerParams(dimension_semantics=sem, vmem_limit_bytes=VMEM_LIMIT)


def _dot(a, b):
    return jnp.dot(a, b, preferred_element_type=F32)


def _dot_nt(a, b):
    return lax.dot_general(a, b, (((1,), (1,)), ((), ())), preferred_element_type=F32)


def _dot_tn(a, b):
    return lax.dot_general(a, b, (((0,), (0,)), ((), ())), preferred_element_type=F32)


def _split(x):
    hi = x.astype(BF16)
    lo = (x - hi.astype(F32)).astype(BF16)
    return hi, lo


def _sigmoid(x):
    return 1.0 / (1.0 + jnp.exp(-x))


def _adaln_kernel(c_ref, w_ref, b_ref, o_ref):
    c_hi, c_lo = _split(c_ref[...])
    w = w_ref[...]
    w_hi, w_lo = _split(w)
    o_ref[...] = _dot(c_hi, w_hi) + _dot(c_hi, w_lo) + _dot(c_lo, w_hi) + b_ref[...]


def _adaln(c8, ada_w, ada_b):
    d, n = ada_w.shape
    tn = 512
    return pl.pallas_call(
        _adaln_kernel,
        grid=(n // tn,),
        in_specs=[pl.BlockSpec((8, d), lambda j: (0, 0)),
                  pl.BlockSpec((d, tn), lambda j: (0, j)),
                  pl.BlockSpec((1, tn), lambda j: (0, j))],
        out_specs=pl.BlockSpec((8, tn), lambda j: (0, j)),
        out_shape=jax.ShapeDtypeStruct((8, n), F32),
        compiler_params=_cparams(("arbitrary",)),
        name="adaln",
    )(c8, ada_w, ada_b)


def _modulated_norm(x, g, scale, shift):
    ms = jnp.mean(x * x, axis=-1, keepdims=True)
    return x * lax.rsqrt(ms + NORM_EPS) * g * (1.0 + scale) + shift


def _inproj_kernel(x_ref, g_ref, scale_ref, shift_ref, w_ref, wl_ref,
                   p1_ref, pq_ref, pg_ref, pkv_ref, p2_ref, h_ref, *, n1, nq, ng):
    j = pl.program_id(2)

    @pl.when(j == 0)
    def _():
        h = _modulated_norm(x_ref[0], g_ref[...], scale_ref[0], shift_ref[0])
        hb = h.astype(BF16)
        h_ref[...] = hb
        p2_ref[0] = _dot(hb, wl_ref[...])

    acc = _dot(h_ref[...], w_ref[...]).astype(BF16)

    @pl.when(j < n1)
    def _():
        p1_ref[0] = acc

    @pl.when((j >= n1) & (j < n1 + nq))
    def _():
        pq_ref[0] = acc

    @pl.when((j >= n1 + nq) & (j < n1 + nq + ng))
    def _():
        pg_ref[0] = acc

    @pl.when(j >= n1 + nq + ng)
    def _():
        pkv_ref[0] = acc


def _inproj(x, g, scale, shift, w_main, w_lora, tm=1024, tn=512):
    b, t, d = x.shape
    n_main = w_main.shape[1]
    n1, nq, ng, nkv = 4 * RW_WIDTH // tn, AT_WIDTH // tn, AT_WIDTH // tn, 2 * AT_KV_WIDTH // tn
    assert n_main == (n1 + nq + ng + nkv) * tn
    clampi = lambda lo, n: (lambda bb, i, j: (bb, i, jnp.clip(j - lo, 0, n - 1)))
    kern = functools.partial(_inproj_kernel, n1=n1, nq=nq, ng=ng)
    return pl.pallas_call(
        kern,
        grid=(b, t // tm, n_main // tn),
        in_specs=[pl.BlockSpec((1, tm, d), lambda bb, i, j: (bb, i, 0)),
                  pl.BlockSpec((1, d), lambda bb, i, j: (0, 0)),
                  pl.BlockSpec((1, 1, d), lambda bb, i, j: (bb, 0, 0)),
                  pl.BlockSpec((1, 1, d), lambda bb, i, j: (bb, 0, 0)),
                  pl.BlockSpec((d, tn), lambda bb, i, j: (0, j)),
                  pl.BlockSpec((d, LORA_COLS), lambda bb, i, j: (0, 0))],
        out_specs=[pl.BlockSpec((1, tm, tn), clampi(0, n1)),
                   pl.BlockSpec((1, tm, tn), clampi(n1, nq)),
                   pl.BlockSpec((1, tm, tn), clampi(n1 + nq, ng)),
                   pl.BlockSpec((1, tm, tn), clampi(n1 + nq + ng, nkv)),
                   pl.BlockSpec((1, tm, LORA_COLS), lambda bb, i, j: (bb, i, 0))],
        out_shape=[jax.ShapeDtypeStruct((b, t, 4 * RW_WIDTH), BF16),
                   jax.ShapeDtypeStruct((b, t, AT_WIDTH), BF16),
                   jax.ShapeDtypeStruct((b, t, AT_WIDTH), BF16),
                   jax.ShapeDtypeStruct((b, t, 2 * AT_KV_WIDTH), BF16),
                   jax.ShapeDtypeStruct((b, t, LORA_COLS), F32)],
        scratch_shapes=[pltpu.VMEM((tm, d), BF16)],
        compiler_params=_cparams(("arbitrary", "arbitrary", "arbitrary")),
        name="inproj",
    )(x, g, scale, shift, w_main, w_lora)


def _seg_sums(xs, bd):
    rows, width = xs[0].shape
    groups = width // SEG_LANES
    pieces = []
    for x in xs:
        for part in _split(x):
            for g in range(groups):
                pieces.append(part[:, g * SEG_LANES:(g + 1) * SEG_LANES])
    res = _dot(jnp.concatenate(pieces, axis=0), bd)
    outs = []
    for n in range(len(xs)):
        base = n * 2 * groups * rows
        hi = jnp.concatenate([res[base + g * rows: base + (g + 1) * rows] for g in range(groups)], axis=1)
        base += groups * rows
        lo = jnp.concatenate([res[base + g * rows: base + (g + 1) * rows] for g in range(groups)], axis=1)
        outs.append(hi + lo)
    return outs


def _block_diag_ones(width, seg):
    r = lax.broadcasted_iota(jnp.int32, (width, width), 0) // seg
    c = lax.broadcasted_iota(jnp.int32, (width, width), 1) // seg
    return jnp.where(r == c, 1.0, 0.0).astype(BF16)


def _rwkv_kernel(p1_ref, p2_ref, mu1_ref, mu2_ref, w0_ref, w2h_ref, w2l_ref,
                 a0_ref, a2h_ref, a2l_ref, kk_ref, ka_ref, rk_ref, lnw_ref, lnb_ref, bd_ref,
                 y_ref, state_ref, carry1_ref, carry2_ref, ycat_ref):
    i = pl.program_id(1)
    C, N, H, W = CHUNK, RW_HEAD_DIM, RW_HEADS, RW_WIDTH

    @pl.when(i == 0)
    def _():
        state_ref[...] = jnp.zeros_like(state_ref)
        carry1_ref[...] = jnp.zeros_like(carry1_ref)
        carry2_ref[...] = jnp.zeros_like(carry2_ref)

    row = lax.broadcasted_iota(jnp.int32, (C, 1), 0)
    first = row == 0

    def shifted(p, carry_ref, mu):
        prev = jnp.where(first, carry_ref[0:1, :], pltpu.roll(p, 1, 0))
        carry_ref[0:1, :] = p[C - 1:C, :]
        return p + (prev - p) * mu

    ps = shifted(p1_ref[0].astype(F32), carry1_ref, mu1_ref[...])
    pl2 = shifted(p2_ref[0], carry2_ref, mu2_ref[...])
    r, k, v, gate = (ps[:, n * W:(n + 1) * W] for n in range(4))
    wd, ad = pl2[:, :DECAY_LORA], pl2[:, DECAY_LORA:]

    def lora(x, wh, wl):
        xh, xl = _split(x)
        return _dot(xh, wh) + _dot(xh, wl) + _dot(xl, wh)

    lw = -math.exp(-0.5) * _sigmoid(w0_ref[...] + lora(jnp.tanh(wd), w2h_ref[...], w2l_ref[...]))
    a = _sigmoid(a0_ref[...] + lora(ad, a2h_ref[...], a2l_ref[...]))

    bd = bd_ref[...]
    kk = k * kk_ref[...]
    k2 = k * (1.0 + (a - 1.0) * ka_ref[...])
    kk_ss, bonus = _seg_sums([kk * kk, r * k2 * rk_ref[...]], bd)
    kk = kk / jnp.maximum(jnp.sqrt(kk_ss), 1e-12)
    av = -kk
    bv = kk * a

    rr = lax.broadcasted_iota(jnp.int32, (C, C), 0)
    cc = lax.broadcasted_iota(jnp.int32, (C, C), 1)
    strict = rr > cc
    incl = rr >= cc
    tril = jnp.where(incl, 1.0, 0.0).astype(BF16)
    incl2 = lax.broadcasted_iota(jnp.int32, (C, 2 * C), 0) >= (lax.broadcasted_iota(jnp.int32, (C, 2 * C), 1) & (C - 1))
    lw_hi, lw_lo = _split(lw)
    G = _dot(tril, lw_hi) + _dot(tril, lw_lo)
    g_end = G[C - 1:C, :]
    e_g = jnp.exp(G)
    e_ng = jnp.exp(-G)
    e_end = jnp.exp(g_end - G)
    gamma = jnp.exp(g_end)
    Rt = (r * e_g).astype(BF16)
    At = (av * jnp.exp(G - lw)).astype(BF16)
    Kt = (k2 * e_ng).astype(BF16)
    Bt = (bv * e_ng).astype(BF16)
    Bh = (bv * e_end).astype(BF16)
    Kh = (k2 * e_end).astype(BF16)
    vb = v.astype(BF16)
    Rt32 = r * e_g

    eye_c = jnp.where(rr == cc, 1.0, 0.0)
    rn = lax.broadcasted_iota(jnp.int32, (N, N), 0)
    cn = lax.broadcasted_iota(jnp.int32, (N, N), 1)
    eye_n = rn == cn
    zeros_cn = jnp.zeros((C, N), BF16)

    for h in range(H):
        sl = slice(h * N, (h + 1) * N)
        At_h, V_h = At[:, sl], vb[:, sl]
        P = _dot_nt(jnp.concatenate([At_h, Rt[:, sl]], axis=0),
                    jnp.concatenate([Bt[:, sl], Kt[:, sl]], axis=0))
        L = jnp.where(strict, P[:C, :C], 0.0)
        Aak = jnp.where(strict, P[:C, C:], 0.0).astype(BF16)
        Arbk = jnp.where(incl2, P[C:, :], 0.0).astype(BF16)
        Lb = L.astype(BF16)
        X = eye_c + L
        Pk = _dot(Lb, Lb)
        steps = int(math.log2(C)) - 1
        for lvl in range(steps):
            Pb = Pk.astype(BF16)
            if lvl + 1 < steps:
                XP = _dot(Pb, jnp.concatenate([X.astype(BF16), Pb], axis=1))
                X = X + XP[:, :C]
                Pk = XP[:, C:]
            else:
                X = X + _dot(Pb, X.astype(BF16))
        AakV = _dot(Aak, V_h).astype(BF16)
        WU = _dot(X.astype(BF16), jnp.concatenate([At_h, AakV], axis=1)).astype(BF16)
        rhs2 = jnp.concatenate([WU, jnp.concatenate([zeros_cn, V_h], axis=1)], axis=0)
        top = _dot(Arbk, rhs2)
        bot = _dot_tn(jnp.concatenate([Bh[:, sl], Kh[:, sl]], axis=0), rhs2)
        Qt = (Rt32[:, sl] + top[:, :N]).astype(BF16)
        Phi = (bot[:, :N] + jnp.where(eye_n, gamma[:, sl], 0.0)).astype(BF16)
        ST = state_ref[h]
        STb = ST.astype(BF16)
        ycat_ref[:, sl] = top[:, N:] + _dot(Qt, STb)
        state_ref[h] = _dot(Phi, STb) + bot[:, N:]

    y = ycat_ref[...]
    mean, = _seg_sums([y], bd)
    yc = y - mean * (1.0 / N)
    var, = _seg_sums([yc * yc], bd)
    yn = yc * lax.rsqrt(var * (1.0 / N) + LNX_EPS) * lnw_ref[...] + lnb_ref[...]
    out = (yn + bonus * v) * (gate * _sigmoid(gate))
    y_ref[0] = out.astype(y_ref.dtype)


def _rwkv(p1, p2, mu1, mu2, w0, w2h, w2l, a0, a2h, a2l, k_k, k_a, r_k, lnx_w, lnx_b, bd):
    b, t, _ = p1.shape
    C, W = CHUNK, RW_WIDTH
    vec = lambda n: pl.BlockSpec((1, n), lambda bb, i: (0, 0))
    mat = lambda r_, c_: pl.BlockSpec((r_, c_), lambda bb, i: (0, 0))
    return pl.pallas_call(
        _rwkv_kernel,
        grid=(b, t // C),
        in_specs=[pl.BlockSpec((1, C, 4 * W), lambda bb, i: (bb, i, 0)),
                  pl.BlockSpec((1, C, LORA_COLS), lambda bb, i: (bb, i, 0)),
                  vec(4 * W), vec(LORA_COLS), vec(W), mat(DECAY_LORA, W), mat(DECAY_LORA, W),
                  vec(W), mat(ICLR_LORA, W), mat(ICLR_LORA, W), vec(W), vec(W), vec(W), vec(W), vec(W),
                  mat(SEG_LANES, SEG_LANES)],
        out_specs=pl.BlockSpec((1, C, W), lambda bb, i: (bb, i, 0)),
        out_shape=jax.ShapeDtypeStruct((b, t, W), BF16),
        scratch_shapes=[pltpu.VMEM((RW_HEADS, RW_HEAD_DIM, RW_HEAD_DIM), F32),
                        pltpu.VMEM((8, 4 * W), F32),
                        pltpu.VMEM((8, LORA_COLS), F32),
                        pltpu.VMEM((C, W), F32)],
        compiler_params=_cparams(("arbitrary", "arbitrary")),
        name="rwkv7",
    )(p1, p2, mu1, mu2, w0, w2h, w2l, a0, a2h, a2l, k_k, k_a, r_k, lnx_w, lnx_b, bd)


def _swa_kernel(q_ref, g_ref, kv_ref, kvp_ref, qn_ref, kn_ref, sink_ref, bd_ref, o_ref, ocat_ref):
    n = pl.program_id(1)
    HD, KVW = AT_HEAD_DIM, AT_KV_WIDTH
    bd = bd_ref[...]
    q = q_ref[0].astype(F32)
    kv = kv_ref[0].astype(F32)
    kvp = kvp_ref[0].astype(F32)
    k_all = jnp.concatenate([kvp[:, :KVW], kv[:, :KVW]], axis=0)
    v_all = jnp.concatenate([kvp[:, KVW:], kv[:, KVW:]], axis=0).astype(BF16)
    q_ss, = _seg_sums([q * q], bd)
    k_ss, = _seg_sums([k_all * k_all], bd)
    qn = (q * lax.rsqrt(q_ss * (1.0 / HD) + NORM_EPS) * qn_ref[...] * (HD ** -0.5)).astype(BF16)
    kn = (k_all * lax.rsqrt(k_ss * (1.0 / HD) + NORM_EPS) * kn_ref[...]).astype(BF16)

    qi = lax.broadcasted_iota(jnp.int32, (BLOCK, 2 * BLOCK), 0)
    si = lax.broadcasted_iota(jnp.int32, (BLOCK, 2 * BLOCK), 1)
    diff = qi + BLOCK - si
    allowed = (diff >= 0) & (diff < WINDOW) & (n * BLOCK + si - BLOCK >= 0)

    for hq in range(AT_HEADS):
        kh = hq // AT_GROUP
        qs = slice(hq * HD, (hq + 1) * HD)
        ks = slice(kh * HD, (kh + 1) * HD)
        s = jnp.where(allowed, _dot_nt(qn[:, qs], kn[:, ks]), NEG_INF)
        sink = sink_ref[hq]
        m = jnp.maximum(jnp.max(s, axis=-1, keepdims=True), sink)
        p = jnp.exp(s - m)
        denom = jnp.sum(p, axis=-1, keepdims=True) + jnp.exp(sink - m)
        o = _dot(p.astype(BF16), v_all[:, ks])
        ocat_ref[:, qs] = o / denom

    gate = g_ref[0].astype(F32)
    o_ref[0] = (ocat_ref[...] * (gate * _sigmoid(gate))).astype(o_ref.dtype)


def _swa(pq, pg, pkv, q_norm_t, k_norm_t, sinks_t, bd):
    b, t, _ = pq.shape
    nb = t // BLOCK
    return pl.pallas_call(
        _swa_kernel,
        grid=(b, nb),
        in_specs=[pl.BlockSpec((1, BLOCK, AT_WIDTH), lambda bb, n: (bb, n, 0)),
                  pl.BlockSpec((1, BLOCK, AT_WIDTH), lambda bb, n: (bb, n, 0)),
                  pl.BlockSpec((1, BLOCK, 2 * AT_KV_WIDTH), lambda bb, n: (bb, n, 0)),
                  pl.BlockSpec((1, BLOCK, 2 * AT_KV_WIDTH), lambda bb, n: (bb, jnp.maximum(n - 1, 0), 0)),
                  pl.BlockSpec((1, AT_WIDTH), lambda bb, n: (0, 0)),
                  pl.BlockSpec((1, AT_KV_WIDTH), lambda bb, n: (0, 0)),
                  pl.BlockSpec(memory_space=pltpu.SMEM),
                  pl.BlockSpec((SEG_LANES, SEG_LANES), lambda bb, n: (0, 0))],
        out_specs=pl.BlockSpec((1, BLOCK, AT_WIDTH), lambda bb, n: (bb, n, 0)),
        out_shape=jax.ShapeDtypeStruct((b, t, AT_WIDTH), BF16),
        scratch_shapes=[pltpu.VMEM((BLOCK, AT_WIDTH), F32)],
        compiler_params=_cparams(("arbitrary", "arbitrary")),
        name="swa",
    )(pq, pg, pkv, pkv, q_norm_t, k_norm_t, sinks_t, bd)


def _merge_kernel(x_ref, g_ref, scale_ref, shift_ref, gate_ref, yr_ref, ya_ref,
                  wgr_ref, wga_ref, wur_ref, wua_ref, wo_ref, o_ref, h_ref, m_ref, *, nj, tn):
    j = pl.program_id(2)

    @pl.when(j == 0)
    def _():
        h = _modulated_norm(x_ref[0], g_ref[...], scale_ref[0], shift_ref[0])
        h_ref[...] = h.astype(BF16)

    @pl.when(j < nj)
    def _():
        hb = h_ref[...]
        g_r = _sigmoid(_dot(hb, wgr_ref[...]))
        g_a = _sigmoid(_dot(hb, wga_ref[...]))
        m = g_r * _dot(yr_ref[0], wur_ref[...]) + g_a * _dot(ya_ref[0], wua_ref[...])
        col = pl.multiple_of(j * tn, tn)
        m_ref[:, pl.ds(col, tn)] = m.astype(BF16)

    @pl.when(j >= nj)
    def _():
        col = pl.multiple_of((j - nj) * tn, tn)
        proj = _dot(m_ref[...], wo_ref[...])
        o_ref[0] = x_ref[0, :, pl.ds(col, tn)] + gate_ref[0, :, pl.ds(col, tn)] * proj


def _merge(x, g, scale, shift, gate, y_r, y_a, wg_r, wg_a, wu_r, wu_a, w_o, tm=512, tn=512):
    b, t, d = x.shape
    nj = d // tn
    first = lambda bb, i, j: (0, jnp.minimum(j, nj - 1))
    second = lambda bb, i, j: (0, jnp.maximum(j - nj, 0))
    row = lambda bb, i, j: (bb, i, 0)
    per_b = lambda bb, i, j: (bb, 0, 0)
    kern = functools.partial(_merge_kernel, nj=nj, tn=tn)
    return pl.pallas_call(
        kern,
        grid=(b, t // tm, 2 * nj),
        in_specs=[pl.BlockSpec((1, tm, d), row),
                  pl.BlockSpec((1, d), lambda bb, i, j: (0, 0)),
                  pl.BlockSpec((1, 1, d), per_b),
                  pl.BlockSpec((1, 1, d), per_b),
                  pl.BlockSpec((1, 1, d), per_b),
                  pl.BlockSpec((1, tm, RW_WIDTH), row),
                  pl.BlockSpec((1, tm, AT_WIDTH), row),
                  pl.BlockSpec((d, tn), first),
                  pl.BlockSpec((d, tn), first),
                  pl.BlockSpec((RW_WIDTH, tn), first),
                  pl.BlockSpec((AT_WIDTH, tn), first),
                  pl.BlockSpec((d, tn), second)],
        out_specs=pl.BlockSpec((1, tm, tn), lambda bb, i, j: (bb, i, jnp.maximum(j - nj, 0))),
        out_shape=jax.ShapeDtypeStruct((b, t, d), F32),
        scratch_shapes=[pltpu.VMEM((tm, d), BF16), pltpu.VMEM((tm, d), BF16)],
        compiler_params=_cparams(("arbitrary", "arbitrary", "arbitrary")),
        name="merge",
    )(x, g, scale, shift, gate, y_r, y_a, wg_r, wg_a, wu_r, wu_a, w_o)


def _layer(x, c, ada_w, ada_b, norm_g, w_in, mu_shift, w0, w2, a0, a2, k_k, k_a, r_k,
           lnx_w, lnx_b, q_norm, k_norm, sinks, w_up_r, w_up_a, w_o):
    b, t, d = x.shape
    W = RW_WIDTH
    c8 = jnp.pad(c, ((0, 8 - b), (0, 0)))
    mod = _adaln(c8, ada_w, ada_b[None, :])[:b]
    shift, scale, gate = (mod[:, None, n * d:(n + 1) * d] for n in range(3))
    g = norm_g[None, :]

    rw_cols = 4 * W + LORA_COLS
    at_cols = 2 * AT_WIDTH + 2 * AT_KV_WIDTH
    w_main = jnp.concatenate([w_in[:, :4 * W], w_in[:, rw_cols:rw_cols + at_cols]], axis=1).astype(BF16)
    w_lora = w_in[:, 4 * W:rw_cols].astype(BF16)
    wg_r = w_in[:, rw_cols + at_cols:rw_cols + at_cols + d].astype(BF16)
    wg_a = w_in[:, rw_cols + at_cols + d:].astype(BF16)

    p1, pq, pg, pkv, p2 = _inproj(x, g, scale, shift, w_main, w_lora)

    bd = _block_diag_ones(SEG_LANES, RW_HEAD_DIM)
    row = lambda a_: a_.reshape(1, -1)
    w2h, w2l = _split(w2)
    a2h, a2l = _split(a2)
    y_r = _rwkv(p1, p2, row(mu_shift[:4 * W]), row(mu_shift[4 * W:]), row(w0), w2h, w2l,
                row(a0), a2h, a2l, row(k_k), row(k_a), row(r_k), row(lnx_w), row(lnx_b), bd)

    y_a = _swa(pq, pg, pkv, row(jnp.tile(q_norm, AT_HEADS)), row(jnp.tile(k_norm, AT_KV_HEADS)),
               sinks, bd)

    return _merge(x, g, scale, shift, gate, y_r, y_a, wg_r, wg_a,
                  w_up_r.astype(BF16), w_up_a.astype(BF16), w_o.astype(BF16))


def kernel(x, c, ada_w, ada_b, norm_g, w_in, mu_shift, w0, w2, a0, a2, k_k, k_a, r_k, lnx_w, lnx_b,
           q_norm, k_norm, sinks, w_up_r, w_up_a, w_o):
    depth = ada_w.shape[0]
    for l in range(depth):
        x = _layer(x, c, ada_w[l], ada_b[l], norm_g[l], w_in[l], mu_shift[l], w0[l], w2[l], a0[l], a2[l],
                   k_k[l], k_a[l], r_k[l], lnx_w[l], lnx_b[l], q_norm[l], k_norm[l], sinks[l],
                   w_up_r[l], w_up_a[l], w_o[l])
    return x
```

```python
import functools
import math

import jax
import jax.numpy as jnp
from jax import lax
from jax.experimental import pallas as pl
from jax.experimental.pallas import tpu as pltpu

F32 = jnp.float32
BF16 = jnp.bfloat16

RW_HEADS = 16
RW_HEAD_DIM = 64
RW_WIDTH = RW_HEADS * RW_HEAD_DIM
DECAY_LORA = 64
ICLR_LORA = 64
LORA_COLS = DECAY_LORA + ICLR_LORA
LNX_EPS = 64e-5
AT_HEADS = 16
AT_KV_HEADS = 4
AT_HEAD_DIM = 64
AT_GROUP = AT_HEADS // AT_KV_HEADS
AT_WIDTH = AT_HEADS * AT_HEAD_DIM
AT_KV_WIDTH = AT_KV_HEADS * AT_HEAD_DIM
WINDOW = 128
BLOCK = 128
NEG_INF = -1e30
NORM_EPS = 1e-6

CHUNK = 64
SEG_LANES = 256
VMEM_LIMIT = 48 * 1024 * 1024


def _cparams(sem):
    return pltpu.CompilerParams(dimension_semantics=sem, vmem_limit_bytes=VMEM_LIMIT)


def _dot(a, b):
    return jnp.dot(a, b, preferred_element_type=F32)


def _dot_nt(a, b):
    return lax.dot_general(a, b, (((1,), (1,)), ((), ())), preferred_element_type=F32)


def _dot_tn(a, b):
    return lax.dot_general(a, b, (((0,), (0,)), ((), ())), preferred_element_type=F32)


def _split(x):
    hi = x.astype(BF16)
    lo = (x - hi.astype(F32)).astype(BF16)
    return hi, lo


def _sigmoid(x):
    return 1.0 / (1.0 + jnp.exp(-x))


def _adaln_kernel(c_ref, w_ref, b_ref, o_ref):
    c_hi, c_lo = _split(c_ref[...])
    w = w_ref[...]
    w_hi, w_lo = _split(w)
    o_ref[...] = _dot(c_hi, w_hi) + _dot(c_hi, w_lo) + _dot(c_lo, w_hi) + b_ref[...]


def _adaln(c8, ada_w, ada_b):
    d, n = ada_w.shape
    tn = 512
    return pl.pallas_call(
        _adaln_kernel,
        grid=(n // tn,),
        in_specs=[pl.BlockSpec((8, d), lambda j: (0, 0)),
                  pl.BlockSpec((d, tn), lambda j: (0, j)),
                  pl.BlockSpec((1, tn), lambda j: (0, j))],
        out_specs=pl.BlockSpec((8, tn), lambda j: (0, j)),
        out_shape=jax.ShapeDtypeStruct((8, n), F32),
        compiler_params=_cparams(("arbitrary",)),
        name="adaln",
    )(c8, ada_w, ada_b)


def _modulated_norm(x, g, scale, shift):
    ms = jnp.mean(x * x, axis=-1, keepdims=True)
    return x * lax.rsqrt(ms + NORM_EPS) * g * (1.0 + scale) + shift


def _inproj_kernel(x_ref, g_ref, scale_ref, shift_ref, w_ref, wl_ref,
                   p1_ref, pq_ref, pg_ref, pkv_ref, p2_ref, h_ref, *, n1, nq, ng):
    j = pl.program_id(2)

    @pl.when(j == 0)
    def _():
        h = _modulated_norm(x_ref[0], g_ref[...], scale_ref[0], shift_ref[0])
        hb = h.astype(BF16)
        h_ref[...] = hb
        p2_ref[0] = _dot(hb, wl_ref[...])

    acc = _dot(h_ref[...], w_ref[...]).astype(BF16)

    @pl.when(j < n1)
    def _():
        p1_ref[0] = acc

    @pl.when((j >= n1) & (j < n1 + nq))
    def _():
        pq_ref[0] = acc

    @pl.when((j >= n1 + nq) & (j < n1 + nq + ng))
    def _():
        pg_ref[0] = acc

    @pl.when(j >= n1 + nq + ng)
    def _():
        pkv_ref[0] = acc


def _inproj(x, g, scale, shift, w_main, w_lora, tm=1024, tn=512):
    b, t, d = x.shape
    n_main = w_main.shape[1]
    n1, nq, ng, nkv = 4 * RW_WIDTH // tn, AT_WIDTH // tn, AT_WIDTH // tn, 2 * AT_KV_WIDTH // tn
    assert n_main == (n1 + nq + ng + nkv) * tn
    clampi = lambda lo, n: (lambda bb, i, j: (bb, i, jnp.clip(j - lo, 0, n - 1)))
    kern = functools.partial(_inproj_kernel, n1=n1, nq=nq, ng=ng)
    return pl.pallas_call(
        kern,
        grid=(b, t // tm, n_main // tn),
        in_specs=[pl.BlockSpec((1, tm, d), lambda bb, i, j: (bb, i, 0)),
                  pl.BlockSpec((1, d), lambda bb, i, j: (0, 0)),
                  pl.BlockSpec((1, 1, d), lambda bb, i, j: (bb, 0, 0)),
                  pl.BlockSpec((1, 1, d), lambda bb, i, j: (bb, 0, 0)),
                  pl.BlockSpec((d, tn), lambda bb, i, j: (0, j)),
                  pl.BlockSpec((d, LORA_COLS), lambda bb, i, j: (0, 0))],
        out_specs=[pl.BlockSpec((1, tm, tn), clampi(0, n1)),
                   pl.BlockSpec((1, tm, tn), clampi(n1, nq)),
                   pl.BlockSpec((1, tm, tn), clampi(n1 + nq, ng)),
                   pl.BlockSpec((1, tm, tn), clampi(n1 + nq + ng, nkv)),
                   pl.BlockSpec((1, tm, LORA_COLS), lambda bb, i, j: (bb, i, 0))],
        out_shape=[jax.ShapeDtypeStruct((b, t, 4 * RW_WIDTH), BF16),
                   jax.ShapeDtypeStruct((b, t, AT_WIDTH), BF16),
                   jax.ShapeDtypeStruct((b, t, AT_WIDTH), BF16),
                   jax.ShapeDtypeStruct((b, t, 2 * AT_KV_WIDTH), BF16),
                   jax.ShapeDtypeStruct((b, t, LORA_COLS), F32)],
        scratch_shapes=[pltpu.VMEM((tm, d), BF16)],
        compiler_params=_cparams(("arbitrary", "arbitrary", "arbitrary")),
        name="inproj",
    )(x, g, scale, shift, w_main, w_lora)


def _seg_sums(xs, bd):
    rows, width = xs[0].shape
    groups = width // SEG_LANES
    pieces = []
    for x in xs:
        for part in _split(x):
            for g in range(groups):
                pieces.append(part[:, g * SEG_LANES:(g + 1) * SEG_LANES])
    res = _dot(jnp.concatenate(pieces, axis=0), bd)
    outs = []
    for n in range(len(xs)):
        base = n * 2 * groups * rows
        hi = jnp.concatenate([res[base + g * rows: base + (g + 1) * rows] for g in range(groups)], axis=1)
        base += groups * rows
        lo = jnp.concatenate([res[base + g * rows: base + (g + 1) * rows] for g in range(groups)], axis=1)
        outs.append(hi + lo)
    return outs


def _block_diag_ones(width, seg):
    r = lax.broadcasted_iota(jnp.int32, (width, width), 0) // seg
    c = lax.broadcasted_iota(jnp.int32, (width, width), 1) // seg
    return jnp.where(r == c, 1.0, 0.0).astype(BF16)


def _rwkv_kernel(p1_ref, p2_ref, mu1_ref, mu2_ref, w0_ref, w2h_ref, w2l_ref,
                 a0_ref, a2h_ref, a2l_ref, kk_ref, ka_ref, rk_ref, lnw_ref, lnb_ref, bd_ref,
                 y_ref, state_ref, carry1_ref, carry2_ref):
    i = pl.program_id(1)
    C, N, H, W = CHUNK, RW_HEAD_DIM, RW_HEADS, RW_WIDTH

    @pl.when(i == 0)
    def _():
        state_ref[...] = jnp.zeros_like(state_ref)
        carry1_ref[...] = jnp.zeros_like(carry1_ref)
        carry2_ref[...] = jnp.zeros_like(carry2_ref)

    row = lax.broadcasted_iota(jnp.int32, (C, 1), 0)
    first = row == 0

    def shifted(p, carry_ref, mu):
        prev = jnp.where(first, carry_ref[0:1, :], pltpu.roll(p, 1, 0))
        carry_ref[0:1, :] = p[C - 1:C, :]
        return p + (prev - p) * mu

    ps = shifted(p1_ref[0].astype(F32), carry1_ref, mu1_ref[...])
    pl2 = shifted(p2_ref[0], carry2_ref, mu2_ref[...])
    r, k, v, gate = (ps[:, n * W:(n + 1) * W] for n in range(4))
    wd, ad = pl2[:, :DECAY_LORA], pl2[:, DECAY_LORA:]

    def lora(x, wh, wl):
        xh, xl = _split(x)
        return _dot(xh, wh) + _dot(xh, wl) + _dot(xl, wh)

    lw = -math.exp(-0.5) * _sigmoid(w0_ref[...] + lora(jnp.tanh(wd), w2h_ref[...], w2l_ref[...]))
    a = _sigmoid(a0_ref[...] + lora(ad, a2h_ref[...], a2l_ref[...]))

    bd = bd_ref[...]
    kk = k * kk_ref[...]
    k2 = k * (1.0 + (a - 1.0) * ka_ref[...])
    kk_ss, bonus = _seg_sums([kk * kk, r * k2 * rk_ref[...]], bd)
    kk = kk / jnp.maximum(jnp.sqrt(kk_ss), 1e-12)
    av = -kk
    bv = kk * a

    tril = jnp.where(lax.broadcasted_iota(jnp.int32, (C, C), 0) >= lax.broadcasted_iota(jnp.int32, (C, C), 1),
                     1.0, 0.0).astype(BF16)
    lw_hi, lw_lo = _split(lw)
    G = _dot(tril, lw_hi) + _dot(tril, lw_lo)
    g_end = G[C - 1:C, :]
    e_g = jnp.exp(G)
    e_ng = jnp.exp(-G)
    e_end = jnp.exp(g_end - G)
    gamma = jnp.exp(g_end)
    Rt32 = r * e_g
    Rt = Rt32.astype(BF16)
    At = (av * jnp.exp(G - lw)).astype(BF16)
    Kt = (k2 * e_ng).astype(BF16)
    Bt = (bv * e_ng).astype(BF16)
    Bh = (bv * e_end).astype(BF16)
    Kh = (k2 * e_end).astype(BF16)
    vb = v.astype(BF16)

    PAIR = 2 * N
    n_pairs = H // 2
    lane = lax.broadcasted_iota(jnp.int32, (C, PAIR), 1)
    rowc = lax.broadcasted_iota(jnp.int32, (C, PAIR), 0)
    lo = lane < N
    lo2 = lax.broadcasted_iota(jnp.int32, (2 * C, PAIR), 1) < N
    lo_wide = (lax.broadcasted_iota(jnp.int32, (C, 2 * PAIR), 1) & (PAIR - 1)) < N
    strict2 = rowc > (lane & (N - 1))
    incl2 = rowc >= (lane & (N - 1))
    eye_lo = jnp.where(rowc == lane, 1.0, 0.0)
    rp = lax.broadcasted_iota(jnp.int32, (PAIR, PAIR), 0)
    cp = lax.broadcasted_iota(jnp.int32, (PAIR, PAIR), 1)
    same_head = (rp >= N) == (cp >= N)
    eye_p = rp == cp
    zeros_c = jnp.zeros((C, PAIR), BF16)
    zeros_w = jnp.zeros((C, 2 * PAIR), BF16)
    pairs = [slice(q * PAIR, (q + 1) * PAIR) for q in range(n_pairs)]

    tops, bots = [], []
    for s in pairs:
        lhs = jnp.concatenate([At[:, s], Rt[:, s]], axis=0)
        lhs = jnp.concatenate([jnp.where(lo2, lhs, 0.0), jnp.where(lo2, 0.0, lhs)], axis=0)
        P = _dot_nt(lhs, jnp.concatenate([Kt[:, s], Bt[:, s]], axis=0))
        for e in range(2):
            tops.append(jnp.where(strict2, P[e * 2 * C:e * 2 * C + C], 0.0))
            bots.append(jnp.where(incl2, P[e * 2 * C + C:(e + 1) * 2 * C], 0.0).astype(BF16))

    Zs = [jnp.where(lo, eye_lo, t) for t in tops]
    for _ in range(int(math.log2(C))):
        Zb = [z.astype(BF16) for z in Zs]
        Zs = [_dot(zb, jnp.concatenate([zeros_c, zb], axis=0)) + jnp.where(lo, z, 0.0) for z, zb in zip(Zs, Zb)]

    both = lambda xs, q: jnp.concatenate([xs[2 * q], xs[2 * q + 1]], axis=0)
    pick = lambda mask, o: jnp.where(mask, o[:C], o[C:])
    rng = range(n_pairs)
    AakV = [pick(lo, _dot(both(tops, q).astype(BF16), jnp.concatenate([vb[:, pairs[q]], zeros_c], axis=0)))
            .astype(BF16) for q in rng]
    WU = [pick(lo_wide, _dot(both(Zs, q).astype(BF16),
                             jnp.concatenate([jnp.concatenate([At[:, pairs[q]], AakV[q]], axis=1), zeros_w], axis=0)))
          .astype(BF16) for q in rng]
    rhs2 = [jnp.concatenate([jnp.concatenate([zeros_c, vb[:, pairs[q]]], axis=1), WU[q]], axis=0) for q in rng]
    top = [pick(lo_wide, _dot(both(bots, q), rhs2[q])) for q in rng]
    bot = [_dot_tn(jnp.concatenate([Kh[:, pairs[q]], Bh[:, pairs[q]]], axis=0), rhs2[q]) for q in rng]
    Qt = [(Rt32[:, pairs[q]] + top[q][:, :PAIR]).astype(BF16) for q in rng]
    Phi = [(jnp.where(same_head, bot[q][:, :PAIR], 0.0) + jnp.where(eye_p, gamma[:, pairs[q]], 0.0)).astype(BF16)
           for q in rng]
    Psi = [jnp.where(same_head, bot[q][:, PAIR:], 0.0) for q in rng]
    Sb = [state_ref[q].astype(BF16) for q in rng]
    y = jnp.concatenate([top[q][:, PAIR:] + _dot(Qt[q], Sb[q]) for q in rng], axis=1)
    for q in rng:
        state_ref[q] = _dot(Phi[q], Sb[q]) + Psi[q]

    mean, = _seg_sums([y], bd)
    yc = y - mean * (1.0 / N)
    var, = _seg_sums([yc * yc], bd)
    yn = yc * lax.rsqrt(var * (1.0 / N) + LNX_EPS) * lnw_ref[...] + lnb_ref[...]
    out = (yn + bonus * v) * (gate * _sigmoid(gate))
    y_ref[0] = out.astype(y_ref.dtype)


def _rwkv(p1, p2, mu1, mu2, w0, w2h, w2l, a0, a2h, a2l, k_k, k_a, r_k, lnx_w, lnx_b, bd):
    b, t, _ = p1.shape
    C, W = CHUNK, RW_WIDTH
    vec = lambda n: pl.BlockSpec((1, n), lambda bb, i: (0, 0))
    mat = lambda r_, c_: pl.BlockSpec((r_, c_), lambda bb, i: (0, 0))
    return pl.pallas_call(
        _rwkv_kernel,
        grid=(b, t // C),
        in_specs=[pl.BlockSpec((1, C, 4 * W), lambda bb, i: (bb, i, 0)),
                  pl.BlockSpec((1, C, LORA_COLS), lambda bb, i: (bb, i, 0)),
                  vec(4 * W), vec(LORA_COLS), vec(W), mat(DECAY_LORA, W), mat(DECAY_LORA, W),
                  vec(W), mat(ICLR_LORA, W), mat(ICLR_LORA, W), vec(W), vec(W), vec(W), vec(W), vec(W),
                  mat(SEG_LANES, SEG_LANES)],
        out_specs=pl.BlockSpec((1, C, W), lambda bb, i: (bb, i, 0)),
        out_shape=jax.ShapeDtypeStruct((b, t, W), BF16),
        scratch_shapes=[pltpu.VMEM((RW_HEADS // 2, 2 * RW_HEAD_DIM, 2 * RW_HEAD_DIM), F32),
                        pltpu.VMEM((8, 4 * W), F32),
                        pltpu.VMEM((8, LORA_COLS), F32)],
        compiler_params=_cparams(("arbitrary", "arbitrary")),
        name="rwkv7",
    )(p1, p2, mu1, mu2, w0, w2h, w2l, a0, a2h, a2l, k_k, k_a, r_k, lnx_w, lnx_b, bd)


def _swa_kernel(q_ref, g_ref, kv_ref, kvp_ref, qn_ref, kn_ref, sink_ref, bd_ref, o_ref, ocat_ref):
    n = pl.program_id(1)
    HD, KVW = AT_HEAD_DIM, AT_KV_WIDTH
    bd = bd_ref[...]
    q = q_ref[0].astype(F32)
    kv = kv_ref[0].astype(F32)
    kvp = kvp_ref[0].astype(F32)
    k_all = jnp.concatenate([kvp[:, :KVW], kv[:, :KVW]], axis=0)
    v_all = jnp.concatenate([kvp[:, KVW:], kv[:, KVW:]], axis=0).astype(BF16)
    q_ss, = _seg_sums([q * q], bd)
    k_ss, = _seg_sums([k_all * k_all], bd)
    qn = (q * lax.rsqrt(q_ss * (1.0 / HD) + NORM_EPS) * qn_ref[...] * (HD ** -0.5)).astype(BF16)
    kn = (k_all * lax.rsqrt(k_ss * (1.0 / HD) + NORM_EPS) * kn_ref[...]).astype(BF16)

    qi = lax.broadcasted_iota(jnp.int32, (BLOCK, 2 * BLOCK), 0)
    si = lax.broadcasted_iota(jnp.int32, (BLOCK, 2 * BLOCK), 1)
    diff = qi + BLOCK - si
    allowed = (diff >= 0) & (diff < WINDOW) & (n * BLOCK + si - BLOCK >= 0)

    for hq in range(AT_HEADS):
        kh = hq // AT_GROUP
        qs = slice(hq * HD, (hq + 1) * HD)
        ks = slice(kh * HD, (kh + 1) * HD)
        s = jnp.where(allowed, _dot_nt(qn[:, qs], kn[:, ks]), NEG_INF)
        sink = sink_ref[hq]
        m = jnp.maximum(jnp.max(s, axis=-1, keepdims=True), sink)
        p = jnp.exp(s - m)
        denom = jnp.sum(p, axis=-1, keepdims=True) + jnp.exp(sink - m)
        o = _dot(p.astype(BF16), v_all[:, ks])
        ocat_ref[:, qs] = o / denom

    gate = g_ref[0].astype(F32)
    o_ref[0] = (ocat_ref[...] * (gate * _sigmoid(gate))).astype(o_ref.dtype)


def _swa(pq, pg, pkv, q_norm_t, k_norm_t, sinks_t, bd):
    b, t, _ = pq.shape
    nb = t // BLOCK
    return pl.pallas_call(
        _swa_kernel,
        grid=(b, nb),
        in_specs=[pl.BlockSpec((1, BLOCK, AT_WIDTH), lambda bb, n: (bb, n, 0)),
                  pl.BlockSpec((1, BLOCK, AT_WIDTH), lambda bb, n: (bb, n, 0)),
                  pl.BlockSpec((1, BLOCK, 2 * AT_KV_WIDTH), lambda bb, n: (bb, n, 0)),
                  pl.BlockSpec((1, BLOCK, 2 * AT_KV_WIDTH), lambda bb, n: (bb, jnp.maximum(n - 1, 0), 0)),
                  pl.BlockSpec((1, AT_WIDTH), lambda bb, n: (0, 0)),
                  pl.BlockSpec((1, AT_KV_WIDTH), lambda bb, n: (0, 0)),
                  pl.BlockSpec(memory_space=pltpu.SMEM),
                  pl.BlockSpec((SEG_LANES, SEG_LANES), lambda bb, n: (0, 0))],
        out_specs=pl.BlockSpec((1, BLOCK, AT_WIDTH), lambda bb, n: (bb, n, 0)),
        out_shape=jax.ShapeDtypeStruct((b, t, AT_WIDTH), BF16),
        scratch_shapes=[pltpu.VMEM((BLOCK, AT_WIDTH), F32)],
        compiler_params=_cparams(("arbitrary", "arbitrary")),
        name="swa",
    )(pq, pg, pkv, pkv, q_norm_t, k_norm_t, sinks_t, bd)


def _merge_kernel(x_ref, g_ref, scale_ref, shift_ref, gate_ref, yr_ref, ya_ref,
                  wgr_ref, wga_ref, wur_ref, wua_ref, wo_ref, o_ref, h_ref, m_ref, *, nj, tn):
    j = pl.program_id(2)

    @pl.when(j == 0)
    def _():
        h = _modulated_norm(x_ref[0], g_ref[...], scale_ref[0], shift_ref[0])
        h_ref[...] = h.astype(BF16)

    @pl.when(j < nj)
    def _():
        hb = h_ref[...]
        g_r = _sigmoid(_dot(hb, wgr_ref[...]))
        g_a = _sigmoid(_dot(hb, wga_ref[...]))
        m = g_r * _dot(yr_ref[0], wur_ref[...]) + g_a * _dot(ya_ref[0], wua_ref[...])
        col = pl.multiple_of(j * tn, tn)
        m_ref[:, pl.ds(col, tn)] = m.astype(BF16)

    @pl.when(j >= nj)
    def _():
        col = pl.multiple_of((j - nj) * tn, tn)
        proj = _dot(m_ref[...], wo_ref[...])
        o_ref[0] = x_ref[0, :, pl.ds(col, tn)] + gate_ref[0, :, pl.ds(col, tn)] * proj


def _merge(x, g, scale, shift, gate, y_r, y_a, wg_r, wg_a, wu_r, wu_a, w_o, tm=512, tn=512):
    b, t, d = x.shape
    nj = d // tn
    first = lambda bb, i, j: (0, jnp.minimum(j, nj - 1))
    second = lambda bb, i, j: (0, jnp.maximum(j - nj, 0))
    row = lambda bb, i, j: (bb, i, 0)
    per_b = lambda bb, i, j: (bb, 0, 0)
    kern = functools.partial(_merge_kernel, nj=nj, tn=tn)
    return pl.pallas_call(
        kern,
        grid=(b, t // tm, 2 * nj),
        in_specs=[pl.BlockSpec((1, tm, d), row),
                  pl.BlockSpec((1, d), lambda bb, i, j: (0, 0)),
                  pl.BlockSpec((1, 1, d), per_b),
                  pl.BlockSpec((1, 1, d), per_b),
                  pl.BlockSpec((1, 1, d), per_b),
                  pl.BlockSpec((1, tm, RW_WIDTH), row),
                  pl.BlockSpec((1, tm, AT_WIDTH), row),
                  pl.BlockSpec((d, tn), first),
                  pl.BlockSpec((d, tn), first),
                  pl.BlockSpec((RW_WIDTH, tn), first),
                  pl.BlockSpec((AT_WIDTH, tn), first),
                  pl.BlockSpec((d, tn), second)],
        out_specs=pl.BlockSpec((1, tm, tn), lambda bb, i, j: (bb, i, jnp.maximum(j - nj, 0))),
        out_shape=jax.ShapeDtypeStruct((b, t, d), F32),
        scratch_shapes=[pltpu.VMEM((tm, d), BF16), pltpu.VMEM((tm, d), BF16)],
        compiler_params=_cparams(("arbitrary", "arbitrary", "arbitrary")),
        name="merge",
    )(x, g, scale, shift, gate, y_r, y_a, wg_r, wg_a, wu_r, wu_a, w_o)


def _layer(x, c, ada_w, ada_b, norm_g, w_in, mu_shift, w0, w2, a0, a2, k_k, k_a, r_k,
           lnx_w, lnx_b, q_norm, k_norm, sinks, w_up_r, w_up_a, w_o):
    b, t, d = x.shape
    W = RW_WIDTH
    c8 = jnp.pad(c, ((0, 8 - b), (0, 0)))
    mod = _adaln(c8, ada_w, ada_b[None, :])[:b]
    shift, scale, gate = (mod[:, None, n * d:(n + 1) * d] for n in range(3))
    g = norm_g[None, :]

    rw_cols = 4 * W + LORA_COLS
    at_cols = 2 * AT_WIDTH + 2 * AT_KV_WIDTH
    w_main = jnp.concatenate([w_in[:, :4 * W], w_in[:, rw_cols:rw_cols + at_cols]], axis=1).astype(BF16)
    w_lora = w_in[:, 4 * W:rw_cols].astype(BF16)
    wg_r = w_in[:, rw_cols + at_cols:rw_cols + at_cols + d].astype(BF16)
    wg_a = w_in[:, rw_cols + at_cols + d:].astype(BF16)

    p1, pq, pg, pkv, p2 = _inproj(x, g, scale, shift, w_main, w_lora)

    bd = _block_diag_ones(SEG_LANES, RW_HEAD_DIM)
    row = lambda a_: a_.reshape(1, -1)
    w2h, w2l = _split(w2)
    a2h, a2l = _split(a2)
    y_r = _rwkv(p1, p2, row(mu_shift[:4 * W]), row(mu_shift[4 * W:]), row(w0), w2h, w2l,
                row(a0), a2h, a2l, row(k_k), row(k_a), row(r_k), row(lnx_w), row(lnx_b), bd)

    y_a = _swa(pq, pg, pkv, row(jnp.tile(q_norm, AT_HEADS)), row(jnp.tile(k_norm, AT_KV_HEADS)),
               sinks, bd)

    return _merge(x, g, scale, shift, gate, y_r, y_a, wg_r, wg_a,
                  w_up_r.astype(BF16), w_up_a.astype(BF16), w_o.astype(BF16))


def kernel(x, c, ada_w, ada_b, norm_g, w_in, mu_shift, w0, w2, a0, a2, k_k, k_a, r_k, lnx_w, lnx_b,
           q_norm, k_norm, sinks, w_up_r, w_up_a, w_o):
    depth = ada_w.shape[0]
    for l in range(depth):
        x = _layer(x, c, ada_w[l], ada_b[l], norm_g[l], w_in[l], mu_shift[l], w0[l], w2[l], a0[l], a2[l],
                   k_k[l], k_a[l], r_k[l], lnx_w[l], lnx_b[l], q_norm[l], k_norm[l], sinks[l],
                   w_up_r[l], w_up_a[l], w_o[l])
    return x
```

```python
import functools
import math

import jax
import jax.numpy as jnp
from jax import lax
from jax.experimental import pallas as pl
from jax.experimental.pallas import tpu as pltpu

F32 = jnp.float32
BF16 = jnp.bfloat16

RW_HEADS = 16
RW_HEAD_DIM = 64
RW_WIDTH = RW_HEADS * RW_HEAD_DIM
DECAY_LORA = 64
ICLR_LORA = 64
LORA_COLS = DECAY_LORA + ICLR_LORA
LNX_EPS = 64e-5
AT_HEADS = 16
AT_KV_HEADS = 4
AT_HEAD_DIM = 64
AT_GROUP = AT_HEADS // AT_KV_HEADS
AT_WIDTH = AT_HEADS * AT_HEAD_DIM
AT_KV_WIDTH = AT_KV_HEADS * AT_HEAD_DIM
WINDOW = 128
BLOCK = 128
NEG_INF = -1e30
NORM_EPS = 1e-6

CHUNK = 64
SEG_LANES = 256
VMEM_LIMIT = 56 * 1024 * 1024


def _cparams(sem):
    return pltpu.CompilerParams(dimension_semantics=sem, vmem_limit_bytes=VMEM_LIMIT)


def _dot(a, b):
    return jnp.dot(a, b, preferred_element_type=F32)


def _dot_nt(a, b):
    return lax.dot_general(a, b, (((1,), (1,)), ((), ())), preferred_element_type=F32)


def _dot_tn(a, b):
    return lax.dot_general(a, b, (((0,), (0,)), ((), ())), preferred_element_type=F32)


def _split(x):
    hi = x.astype(BF16)
    lo = (x - hi.astype(F32)).astype(BF16)
    return hi, lo


def _sigmoid(x):
    return 0.5 + 0.5 * jnp.tanh(0.5 * x)


def _adaln_kernel(c_ref, w_ref, b_ref, o_ref):
    c_hi, c_lo = _split(c_ref[...])
    w = w_ref[...]
    w_hi, w_lo = _split(w)
    o_ref[...] = _dot(c_hi, w_hi) + _dot(c_hi, w_lo) + _dot(c_lo, w_hi) + b_ref[...]


def _adaln(c8, ada_w, ada_b):
    d, n = ada_w.shape
    tn = 512
    return pl.pallas_call(
        _adaln_kernel,
        grid=(n // tn,),
        in_specs=[pl.BlockSpec((8, d), lambda j: (0, 0)),
                  pl.BlockSpec((d, tn), lambda j: (0, j)),
                  pl.BlockSpec((1, tn), lambda j: (0, j))],
        out_specs=pl.BlockSpec((8, tn), lambda j: (0, j)),
        out_shape=jax.ShapeDtypeStruct((8, n), F32),
        compiler_params=_cparams(("arbitrary",)),
        name="adaln",
    )(c8, ada_w, ada_b)


def _modulated_norm(x, g, scale, shift):
    ms = jnp.mean(x * x, axis=-1, keepdims=True)
    return x * lax.rsqrt(ms + NORM_EPS) * g * (1.0 + scale) + shift


def _inproj_kernel(x_ref, g_ref, scale_ref, shift_ref, w_ref, ws_ref,
                   p_ref, pkv_ref, p2_ref, h_ref, *, sub, n_kv):
    j = pl.program_id(2)

    @pl.when(j == 0)
    def _():
        h = _modulated_norm(x_ref[0], g_ref[...], scale_ref[0], shift_ref[0])
        hb = h.astype(BF16)
        h_ref[0] = hb
        small = _dot(hb, ws_ref[...])
        pkv_ref[0] = small[:, :n_kv].astype(BF16)
        p2_ref[0] = small[:, n_kv:]

    hb = h_ref[0]
    for c in range(0, w_ref.shape[1], sub):
        p_ref[0, :, c:c + sub] = _dot(hb, w_ref[:, c:c + sub]).astype(BF16)


def _inproj(x, g, scale, shift, w_main, w_small, tm=1024, tn=1024, sub=512):
    b, t, d = x.shape
    n_main = w_main.shape[1]
    n_kv = 2 * AT_KV_WIDTH
    n_small = w_small.shape[1]
    row = lambda bb, i, j: (bb, i, 0)
    per_b = lambda bb, i, j: (bb, 0, 0)
    kern = functools.partial(_inproj_kernel, sub=sub, n_kv=n_kv)
    return pl.pallas_call(
        kern,
        grid=(b, t // tm, n_main // tn),
        in_specs=[pl.BlockSpec((1, tm, d), row),
                  pl.BlockSpec((1, d), lambda bb, i, j: (0, 0)),
                  pl.BlockSpec((1, 1, d), per_b),
                  pl.BlockSpec((1, 1, d), per_b),
                  pl.BlockSpec((d, tn), lambda bb, i, j: (0, j)),
                  pl.BlockSpec((d, n_small), lambda bb, i, j: (0, 0))],
        out_specs=[pl.BlockSpec((1, tm, tn), lambda bb, i, j: (bb, i, j)),
                   pl.BlockSpec((1, tm, n_kv), row),
                   pl.BlockSpec((1, tm, n_small - n_kv), row),
                   pl.BlockSpec((1, tm, d), row)],
        out_shape=[jax.ShapeDtypeStruct((b, t, n_main), BF16),
                   jax.ShapeDtypeStruct((b, t, n_kv), BF16),
                   jax.ShapeDtypeStruct((b, t, n_small - n_kv), F32),
                   jax.ShapeDtypeStruct((b, t, d), BF16)],
        compiler_params=_cparams(("arbitrary", "arbitrary", "arbitrary")),
        name="inproj",
    )(x, g, scale, shift, w_main, w_small)


def _seg_sums(xs, bd):
    rows, width = xs[0].shape
    groups = width // SEG_LANES
    pieces = [x.astype(BF16)[:, g * SEG_LANES:(g + 1) * SEG_LANES] for x in xs for g in range(groups)]
    res = _dot(jnp.concatenate(pieces, axis=0), bd)
    return [jnp.concatenate([res[(n * groups + g) * rows:(n * groups + g + 1) * rows] for g in range(groups)], axis=1)
            for n in range(len(xs))]


def _block_diag_ones(width, seg):
    r = lax.broadcasted_iota(jnp.int32, (width, width), 0) // seg
    c = lax.broadcasted_iota(jnp.int32, (width, width), 1) // seg
    return jnp.where(r == c, 1.0, 0.0).astype(BF16)


def _rwkv_kernel(pr_ref, pk_ref, pv_ref, pg_ref, p2_ref, mu1_ref, mu2_ref, w0_ref, w2_ref,
                 a0_ref, a2_ref, kk_ref, ka_ref, rk_ref, lnw_ref, lnb_ref, bd_ref,
                 y_ref, state_ref, carry1_ref, carry2_ref):
    i = pl.program_id(1)
    C, N, H, W = CHUNK, RW_HEAD_DIM, RW_HEADS, RW_WIDTH

    @pl.when(i == 0)
    def _():
        state_ref[...] = jnp.zeros_like(state_ref)
        carry1_ref[...] = jnp.zeros_like(carry1_ref)
        carry2_ref[...] = jnp.zeros_like(carry2_ref)

    row = lax.broadcasted_iota(jnp.int32, (C, 1), 0)
    first = row == 0

    def shifted(p, carry_ref, mu):
        prev = jnp.where(first, carry_ref[0:1, :], pltpu.roll(p, 1, 0))
        carry_ref[0:1, :] = p[C - 1:C, :]
        return p + (prev - p) * mu

    p1 = jnp.concatenate([pr_ref[0], pk_ref[0], pv_ref[0], pg_ref[0]], axis=1)
    ps = shifted(p1.astype(F32), carry1_ref, mu1_ref[...])
    pl2 = shifted(p2_ref[0], carry2_ref, mu2_ref[...])
    r, k, v, gate = (ps[:, n * W:(n + 1) * W] for n in range(4))
    wd, ad = pl2[:, :DECAY_LORA], pl2[:, DECAY_LORA:]

    lw = -math.exp(-0.5) * _sigmoid(w0_ref[...] + _dot(jnp.tanh(wd).astype(BF16), w2_ref[...]))
    a = _sigmoid(a0_ref[...] + _dot(ad.astype(BF16), a2_ref[...]))

    bd = bd_ref[...]
    kk = k * kk_ref[...]
    k2 = k * (1.0 + (a - 1.0) * ka_ref[...])
    kk_ss, bonus = _seg_sums([kk * kk, r * k2 * rk_ref[...]], bd)
    kk = kk / jnp.maximum(jnp.sqrt(kk_ss), 1e-12)
    av = -kk
    bv = kk * a

    tril2 = jnp.where(lax.broadcasted_iota(jnp.int32, (C, 2 * C), 0)
                      >= (lax.broadcasted_iota(jnp.int32, (C, 2 * C), 1) & (C - 1)), 1.0, 0.0).astype(BF16)
    G = _dot(tril2, jnp.concatenate(_split(lw), axis=0))
    g_end = G[C - 1:C, :]
    e_g = jnp.exp(G)
    e_ng = jnp.exp(-G)
    e_end = jnp.exp(g_end - G)
    gamma = jnp.exp(g_end)
    Rt32 = r * e_g
    Rt = Rt32.astype(BF16)
    At = (av * jnp.exp(G - lw)).astype(BF16)
    Kt = (k2 * e_ng).astype(BF16)
    Bt = (bv * e_ng).astype(BF16)
    Bh = (bv * e_end).astype(BF16)
    Kh = (k2 * e_end).astype(BF16)
    vb = v.astype(BF16)

    PAIR = 2 * N
    n_pairs = H // 2
    lane = lax.broadcasted_iota(jnp.int32, (C, PAIR), 1)
    rowc = lax.broadcasted_iota(jnp.int32, (C, PAIR), 0)
    lo = lane < N
    lo2 = lax.broadcasted_iota(jnp.int32, (2 * C, PAIR), 1) < N
    lo_wide = (lax.broadcasted_iota(jnp.int32, (C, 2 * PAIR), 1) & (PAIR - 1)) < N
    strict2 = rowc > (lane & (N - 1))
    incl2 = rowc >= (lane & (N - 1))
    eye_lo = jnp.where(rowc == lane, 1.0, 0.0)
    rp = lax.broadcasted_iota(jnp.int32, (PAIR, PAIR), 0)
    cp = lax.broadcasted_iota(jnp.int32, (PAIR, PAIR), 1)
    same_head = (rp >= N) == (cp >= N)
    eye_p = rp == cp
    zeros_c = jnp.zeros((C, PAIR), BF16)
    zeros_w = jnp.zeros((C, 2 * PAIR), BF16)
    pairs = [slice(q * PAIR, (q + 1) * PAIR) for q in range(n_pairs)]

    tops, bots = [], []
    for s in pairs:
        lhs = jnp.concatenate([At[:, s], Rt[:, s]], axis=0)
        lhs = jnp.concatenate([jnp.where(lo2, lhs, 0.0), jnp.where(lo2, 0.0, lhs)], axis=0)
        P = _dot_nt(lhs, jnp.concatenate([Kt[:, s], Bt[:, s]], axis=0))
        for e in range(2):
            tops.append(jnp.where(strict2, P[e * 2 * C:e * 2 * C + C], 0.0))
            bots.append(jnp.where(incl2, P[e * 2 * C + C:(e + 1) * 2 * C], 0.0).astype(BF16))

    eye_b = eye_lo.astype(BF16)
    Zp = [jnp.concatenate([jnp.where(lo, eye_lo, tops[2 * q]), jnp.where(lo, eye_lo, tops[2 * q + 1])], axis=1)
          .astype(BF16) for q in range(n_pairs)]
    for lvl in range(int(math.log2(C))):
        rhs = [jnp.concatenate([jnp.concatenate([eye_b, zeros_c], axis=1),
                                jnp.concatenate([z[:, :PAIR], zeros_c], axis=1),
                                jnp.concatenate([zeros_c, eye_b], axis=1),
                                jnp.concatenate([zeros_c, z[:, PAIR:]], axis=1)], axis=0) for z in Zp]
        Zp = [_dot(z, w).astype(BF16) for z, w in zip(Zp, rhs)]
    Xs = [jnp.concatenate([z[:, :PAIR], z[:, PAIR:]], axis=0) for z in Zp]

    both = lambda xs, q: jnp.concatenate([xs[2 * q], xs[2 * q + 1]], axis=0)
    pick = lambda mask, o: jnp.where(mask, o[:C], o[C:])
    rng = range(n_pairs)
    AakV = [pick(lo, _dot(both(tops, q).astype(BF16), jnp.concatenate([vb[:, pairs[q]], zeros_c], axis=0)))
            .astype(BF16) for q in rng]
    WU = [pick(lo_wide, _dot(Xs[q],
                             jnp.concatenate([jnp.concatenate([At[:, pairs[q]], AakV[q]], axis=1), zeros_w], axis=0)))
          .astype(BF16) for q in rng]
    rhs2 = [jnp.concatenate([jnp.concatenate([zeros_c, vb[:, pairs[q]]], axis=1), WU[q]], axis=0) for q in rng]
    top = [pick(lo_wide, _dot(both(bots, q), rhs2[q])) for q in rng]
    bot = [_dot_tn(jnp.concatenate([Kh[:, pairs[q]], Bh[:, pairs[q]]], axis=0), rhs2[q]) for q in rng]
    Qt = [(Rt32[:, pairs[q]] + top[q][:, :PAIR]).astype(BF16) for q in rng]
    Phi = [(jnp.where(same_head, bot[q][:, :PAIR], 0.0) + jnp.where(eye_p, gamma[:, pairs[q]], 0.0)).astype(BF16)
           for q in rng]
    Psi = [jnp.where(same_head, bot[q][:, PAIR:], 0.0) for q in rng]
    Sb = [state_ref[q].astype(BF16) for q in rng]
    y = jnp.concatenate([top[q][:, PAIR:] + _dot(Qt[q], Sb[q]) for q in rng], axis=1)
    for q in rng:
        state_ref[q] = _dot(Phi[q], Sb[q]) + Psi[q]

    mean, = _seg_sums([y], bd)
    yc = y - mean * (1.0 / N)
    var, = _seg_sums([yc * yc], bd)
    yn = yc * lax.rsqrt(var * (1.0 / N) + LNX_EPS) * lnw_ref[...] + lnb_ref[...]
    out = (yn + bonus * v) * (gate * _sigmoid(gate))
    y_ref[0] = out.astype(y_ref.dtype)


def _rwkv(p_main, p2, mu1, mu2, w0, w2, a0, a2, k_k, k_a, r_k, lnx_w, lnx_b, bd):
    b, t, _ = p_main.shape
    C, W = CHUNK, RW_WIDTH
    vec = lambda n: pl.BlockSpec((1, n), lambda bb, i: (0, 0))
    mat = lambda r_, c_: pl.BlockSpec((r_, c_), lambda bb, i: (0, 0))
    col = lambda n: pl.BlockSpec((1, C, W), lambda bb, i: (bb, i, n))
    return pl.pallas_call(
        _rwkv_kernel,
        grid=(b, t // C),
        in_specs=[col(0), col(1), col(2), col(3),
                  pl.BlockSpec((1, C, LORA_COLS), lambda bb, i: (bb, i, 0)),
                  vec(4 * W), vec(LORA_COLS), vec(W), mat(DECAY_LORA, W),
                  vec(W), mat(ICLR_LORA, W), vec(W), vec(W), vec(W), vec(W), vec(W),
                  mat(SEG_LANES, SEG_LANES)],
        out_specs=pl.BlockSpec((1, C, W), lambda bb, i: (bb, i, 0)),
        out_shape=jax.ShapeDtypeStruct((b, t, W), BF16),
        scratch_shapes=[pltpu.VMEM((RW_HEADS // 2, 2 * RW_HEAD_DIM, 2 * RW_HEAD_DIM), F32),
                        pltpu.VMEM((8, 4 * W), F32),
                        pltpu.VMEM((8, LORA_COLS), F32)],
        compiler_params=_cparams(("arbitrary", "arbitrary")),
        name="rwkv7",
    )(p_main, p_main, p_main, p_main, p2, mu1, mu2, w0, w2, a0, a2, k_k, k_a, r_k, lnx_w, lnx_b, bd)


def _swa_kernel(q_ref, g_ref, kv_ref, kvp_ref, qn_ref, kn_ref, sink_ref, bd_ref, o_ref, ocat_ref):
    n = pl.program_id(1)
    HD, KVW = AT_HEAD_DIM, AT_KV_WIDTH
    bd = bd_ref[...]
    q = q_ref[0].astype(F32)
    kv = kv_ref[0].astype(F32)
    kvp = kvp_ref[0].astype(F32)
    k_all = jnp.concatenate([kvp[:, :KVW], kv[:, :KVW]], axis=0)
    v_all = jnp.concatenate([kvp[:, KVW:], kv[:, KVW:]], axis=0).astype(BF16)
    q_ss, = _seg_sums([q * q], bd)
    k_ss, = _seg_sums([k_all * k_all], bd)
    qn = (q * lax.rsqrt(q_ss * (1.0 / HD) + NORM_EPS) * qn_ref[...] * (HD ** -0.5)).astype(BF16)
    kn = (k_all * lax.rsqrt(k_ss * (1.0 / HD) + NORM_EPS) * kn_ref[...]).astype(BF16)

    qi = lax.broadcasted_iota(jnp.int32, (BLOCK, 2 * BLOCK), 0)
    si = lax.broadcasted_iota(jnp.int32, (BLOCK, 2 * BLOCK), 1)
    diff = qi + BLOCK - si
    allowed = (diff >= 0) & (diff < WINDOW) & (n * BLOCK + si - BLOCK >= 0)

    for hq in range(AT_HEADS):
        kh = hq // AT_GROUP
        qs = slice(hq * HD, (hq + 1) * HD)
        ks = slice(kh * HD, (kh + 1) * HD)
        s = jnp.where(allowed, _dot_nt(qn[:, qs], kn[:, ks]), NEG_INF)
        sink = sink_ref[hq]
        m = jnp.maximum(jnp.max(s, axis=-1, keepdims=True), sink)
        p = jnp.exp(s - m)
        denom = jnp.sum(p, axis=-1, keepdims=True) + jnp.exp(sink - m)
        o = _dot(p.astype(BF16), v_all[:, ks])
        ocat_ref[:, qs] = o / denom

    gate = g_ref[0].astype(F32)
    o_ref[0] = (ocat_ref[...] * (gate * _sigmoid(gate))).astype(o_ref.dtype)


def _swa(p_main, pkv, q_norm_t, k_norm_t, sinks_t, bd):
    b, t, _ = p_main.shape
    nb = t // BLOCK
    q_blk = 4 * RW_WIDTH // AT_WIDTH
    return pl.pallas_call(
        _swa_kernel,
        grid=(b, nb),
        in_specs=[pl.BlockSpec((1, BLOCK, AT_WIDTH), lambda bb, n: (bb, n, q_blk)),
                  pl.BlockSpec((1, BLOCK, AT_WIDTH), lambda bb, n: (bb, n, q_blk + 1)),
                  pl.BlockSpec((1, BLOCK, 2 * AT_KV_WIDTH), lambda bb, n: (bb, n, 0)),
                  pl.BlockSpec((1, BLOCK, 2 * AT_KV_WIDTH), lambda bb, n: (bb, jnp.maximum(n - 1, 0), 0)),
                  pl.BlockSpec((1, AT_WIDTH), lambda bb, n: (0, 0)),
                  pl.BlockSpec((1, AT_KV_WIDTH), lambda bb, n: (0, 0)),
                  pl.BlockSpec(memory_space=pltpu.SMEM),
                  pl.BlockSpec((SEG_LANES, SEG_LANES), lambda bb, n: (0, 0))],
        out_specs=pl.BlockSpec((1, BLOCK, AT_WIDTH), lambda bb, n: (bb, n, 0)),
        out_shape=jax.ShapeDtypeStruct((b, t, AT_WIDTH), BF16),
        scratch_shapes=[pltpu.VMEM((BLOCK, AT_WIDTH), F32)],
        compiler_params=_cparams(("arbitrary", "arbitrary")),
        name="swa",
    )(p_main, p_main, pkv, pkv, q_norm_t, k_norm_t, sinks_t, bd)


def _merge_kernel(h_ref, x_ref, gate_ref, yr_ref, ya_ref, wgr_ref, wga_ref, wur_ref, wua_ref, wo_ref,
                  o_ref, m_ref, *, nj, tn):
    j = pl.program_id(2)

    @pl.when(j < nj)
    def _():
        hb = h_ref[0]
        g_r = _sigmoid(_dot(hb, wgr_ref[...]))
        g_a = _sigmoid(_dot(hb, wga_ref[...]))
        m = g_r * _dot(yr_ref[0], wur_ref[...]) + g_a * _dot(ya_ref[0], wua_ref[...])
        col = pl.multiple_of(j * tn, tn)
        m_ref[:, pl.ds(col, tn)] = m.astype(BF16)

    @pl.when(j >= nj)
    def _():
        col = pl.multiple_of((j - nj) * tn, tn)
        o_ref[0] = x_ref[0] + gate_ref[0, :, pl.ds(col, tn)] * _dot(m_ref[...], wo_ref[...])


def _merge(h, x, gate, y_r, y_a, wg_r, wg_a, wu_r, wu_a, w_o, tm=1024, tn=512):
    b, t, d = x.shape
    nj = d // tn
    first = lambda bb, i, j: (0, jnp.minimum(j, nj - 1))
    second = lambda bb, i, j: (0, jnp.maximum(j - nj, 0))
    out_tile = lambda bb, i, j: (bb, i, jnp.maximum(j - nj, 0))
    row = lambda bb, i, j: (bb, i, 0)
    kern = functools.partial(_merge_kernel, nj=nj, tn=tn)
    return pl.pallas_call(
        kern,
        grid=(b, t // tm, 2 * nj),
        in_specs=[pl.BlockSpec((1, tm, d), row),
                  pl.BlockSpec((1, tm, tn), out_tile),
                  pl.BlockSpec((1, 1, d), lambda bb, i, j: (bb, 0, 0)),
                  pl.BlockSpec((1, tm, RW_WIDTH), row),
                  pl.BlockSpec((1, tm, AT_WIDTH), row),
                  pl.BlockSpec((d, tn), first),
                  pl.BlockSpec((d, tn), first),
                  pl.BlockSpec((RW_WIDTH, tn), first),
                  pl.BlockSpec((AT_WIDTH, tn), first),
                  pl.BlockSpec((d, tn), second)],
        out_specs=pl.BlockSpec((1, tm, tn), out_tile),
        out_shape=jax.ShapeDtypeStruct((b, t, d), F32),
        scratch_shapes=[pltpu.VMEM((tm, d), BF16)],
        compiler_params=_cparams(("arbitrary", "arbitrary", "arbitrary")),
        name="merge",
    )(h, x, gate, y_r, y_a, wg_r, wg_a, wu_r, wu_a, w_o)


def _layer(x, c, ada_w, ada_b, norm_g, w_in, mu_shift, w0, w2, a0, a2, k_k, k_a, r_k,
           lnx_w, lnx_b, q_norm, k_norm, sinks, w_up_r, w_up_a, w_o):
    b, t, d = x.shape
    W = RW_WIDTH
    c8 = jnp.pad(c, ((0, 8 - b), (0, 0)))
    mod = _adaln(c8, ada_w, ada_b[None, :])[:b]
    shift, scale, gate = (mod[:, None, n * d:(n + 1) * d] for n in range(3))

    rw_cols = 4 * W + LORA_COLS
    kv0 = rw_cols + 2 * AT_WIDTH
    g0 = kv0 + 2 * AT_KV_WIDTH
    w_main = jnp.concatenate([w_in[:, :4 * W], w_in[:, rw_cols:kv0]], axis=1).astype(BF16)
    w_small = jnp.concatenate([w_in[:, kv0:g0], w_in[:, 4 * W:rw_cols]], axis=1).astype(BF16)
    wg_r = w_in[:, g0:g0 + d].astype(BF16)
    wg_a = w_in[:, g0 + d:].astype(BF16)

    p_main, pkv, p2, h = _inproj(x, norm_g[None, :], scale, shift, w_main, w_small)

    bd = _block_diag_ones(SEG_LANES, RW_HEAD_DIM)
    row = lambda a_: a_.reshape(1, -1)
    y_r = _rwkv(p_main, p2, row(mu_shift[:4 * W]), row(mu_shift[4 * W:]), row(w0), w2.astype(BF16),
                row(a0), a2.astype(BF16), row(k_k), row(k_a), row(r_k), row(lnx_w), row(lnx_b), bd)

    y_a = _swa(p_main, pkv, row(jnp.tile(q_norm, AT_HEADS)), row(jnp.tile(k_norm, AT_KV_HEADS)), sinks, bd)

    return _merge(h, x, gate, y_r, y_a, wg_r, wg_a, w_up_r.astype(BF16), w_up_a.astype(BF16), w_o.astype(BF16))


def kernel(x, c, ada_w, ada_b, norm_g, w_in, mu_shift, w0, w2, a0, a2, k_k, k_a, r_k, lnx_w, lnx_b,
           q_norm, k_norm, sinks, w_up_r, w_up_a, w_o):
    depth = ada_w.shape[0]
    for l in range(depth):
        x = _layer(x, c, ada_w[l], ada_b[l], norm_g[l], w_in[l], mu_shift[l], w0[l], w2[l], a0[l], a2[l],
                   k_k[l], k_a[l], r_k[l], lnx_w[l], lnx_b[l], q_norm[l], k_norm[l], sinks[l],
                   w_up_r[l], w_up_a[l], w_o[l])
    return x
```
